```python
import math
import jax, jax.numpy as jnp
from jax import lax
import numpy as np

D_MODEL = 1024
BATCH = 4
SEQ = 4096
DEPTH = 1

CHUNK = 64
Q_BLOCK = 128

MLA_HEADS = 8
MLA_NOPE = 64
MLA_ROPE = 32
MLA_V = 64
MLA_Q_RANK = 256
MLA_KV_RANK = 128
ROPE_THETA = 10000.0
D_MLA = MLA_HEADS * MLA_V

HG_HEADS = 4
HG_DK = 128
HG_DV = 128
D_HG = HG_HEADS * HG_DV

D_MIX = D_MLA + D_HG
IN_SIZES = (MLA_Q_RANK, MLA_KV_RANK, MLA_ROPE, HG_HEADS * HG_DK, HG_HEADS * HG_DV, HG_HEADS * HG_DK, HG_HEADS * HG_DV)
IN_COLS = sum(IN_SIZES)

N_GROUPS = 4
EXPERTS_PER_GROUP = 8
N_EXPERTS = N_GROUPS * EXPERTS_PER_GROUP
TOP_K = 2
D_EXPERT = 512
MOE_BLOCK = 128

RMS_EPS = 1e-6
LN_EPS = 1e-5
ALPHA = (2.0 * DEPTH) ** 0.25
BETA = (8.0 * DEPTH) ** -0.25

kernel_name = "hybrid_mla_hgrn2_hiermoe_deepnorm"


def rms_norm(x, g):
    xf = x.astype(jnp.float32)
    y = xf * lax.rsqrt(jnp.mean(xf * xf, axis=-1, keepdims=True) + RMS_EPS)
    return (y * g.astype(jnp.float32)).astype(x.dtype)


def layer_norm(x, g, b):
    xf = x.astype(jnp.float32)
    mu = jnp.mean(xf, axis=-1, keepdims=True)
    xc = xf - mu
    var = jnp.mean(xc * xc, axis=-1, keepdims=True)
    y = xc * lax.rsqrt(var + LN_EPS) * g.astype(jnp.float32) + b.astype(jnp.float32)
    return y.astype(x.dtype)


def rope(x, cos, sin):
    x1, x2 = jnp.split(x, 2, axis=-1)
    return jnp.concatenate([x1 * cos - x2 * sin, x2 * cos + x1 * sin], axis=-1)


def mla_group(c_q, c_kv, k_r, positions, q_norm_g, w_uq, kv_norm_g, w_ukv):
    B, S, _ = c_q.shape
    dt = c_q.dtype
    inv_freq = ROPE_THETA ** (-jnp.arange(0, MLA_ROPE, 2, dtype=jnp.float32) / MLA_ROPE)
    ang = positions.astype(jnp.float32)[..., None] * inv_freq
    cos = jnp.cos(ang).astype(dt)
    sin = jnp.sin(ang).astype(dt)
    q = (rms_norm(c_q, q_norm_g) @ w_uq).reshape(B, S, MLA_HEADS, MLA_NOPE + MLA_ROPE)
    q_nope = q[..., :MLA_NOPE]
    q_rope = rope(q[..., MLA_NOPE:], cos[:, :, None, :], sin[:, :, None, :])
    kv = (rms_norm(c_kv, kv_norm_g) @ w_ukv).reshape(B, S, MLA_HEADS, MLA_NOPE + MLA_V)
    k_nope = kv[..., :MLA_NOPE]
    v = kv[..., MLA_NOPE:]
    k_rope = rope(k_r, cos, sin)
    scale = (MLA_NOPE + MLA_ROPE) ** -0.5
    nqb = S // Q_BLOCK
    key_chunk = jnp.arange(S, dtype=jnp.int32) // CHUNK

    def one_block(args):
        qn, qr, qb = args
        s = jnp.einsum('bqhd,bkhd->bhqk', qn, k_nope) + jnp.einsum('bqhr,bkr->bhqk', qr, k_rope)
        s = s.astype(jnp.float32) * scale
        q_chunk = (qb * Q_BLOCK + jnp.arange(Q_BLOCK, dtype=jnp.int32)) // CHUNK
        mask = key_chunk[None, :] <= q_chunk[:, None]
        p = jax.nn.softmax(jnp.where(mask, s, -jnp.inf), axis=-1).astype(v.dtype)
        return jnp.einsum('bhqk,bkhd->bqhd', p, v)

    qn_b = q_nope.reshape(B, nqb, Q_BLOCK, MLA_HEADS, MLA_NOPE).transpose(1, 0, 2, 3, 4)
    qr_b = q_rope.reshape(B, nqb, Q_BLOCK, MLA_HEADS, MLA_ROPE).transpose(1, 0, 2, 3, 4)
    out = lax.map(one_block, (qn_b, qr_b, jnp.arange(nqb, dtype=jnp.int32)))
    return out.transpose(1, 0, 2, 3, 4).reshape(B, S, D_MLA)


def hgrn2_group(f_pre, i_in, q_in, g_in, lb_logits, layer, hg_norm_g):
    B, S, _ = f_pre.shape
    dt = f_pre.dtype
    lb = jnp.cumsum(jax.nn.softmax(lb_logits.astype(jnp.float32), axis=0), axis=0)[layer]
    f = lb + (1.0 - lb) * jax.nn.sigmoid(f_pre.astype(jnp.float32))
    log_f = jnp.log(f)
    k = 1.0 - f
    nc = S // CHUNK

    def to_chunks(t, d):
        return t.astype(jnp.float32).reshape(B, nc, CHUNK, HG_HEADS, d).transpose(1, 0, 3, 2, 4)

    qc = to_chunks(q_in, HG_DK)
    kc = to_chunks(k, HG_DK)
    vc = to_chunks(i_in, HG_DV)
    lfc = to_chunks(log_f, HG_DK)
    causal = jnp.tril(jnp.ones((CHUNK, CHUNK), dtype=bool))

    def step(state, inp):
        q, kk, v, lf = inp
        b = jnp.cumsum(lf, axis=2)
        diff = b[:, :, :, None, :] - b[:, :, None, :, :]
        decay = jnp.exp(jnp.where(causal[:, :, None], diff, -jnp.inf))
        a = jnp.einsum('bhtd,bhsd,bhtsd->bhts', q, kk, decay)
        o = jnp.einsum('bhts,bhse->bhte', a, v) + jnp.einsum('bhtd,bhde->bhte', q * jnp.exp(b), state)
        b_last = b[:, :, -1:, :]
        new_state = jnp.exp(b_last[:, :, 0, :])[..., None] * state + jnp.einsum('bhsd,bhse->bhde', kk * jnp.exp(b_last - b), v)
        return new_state, o

    s0 = jnp.zeros((B, HG_HEADS, HG_DK, HG_DV), jnp.float32)
    _, o = lax.scan(step, s0, (qc, kc, vc, lfc))
    o = o.transpose(1, 0, 3, 2, 4).reshape(B, S, HG_HEADS, HG_DV)
    g = g_in.astype(jnp.float32).reshape(B, S, HG_HEADS, HG_DV)
    o = rms_norm(o, hg_norm_g) * jax.nn.silu(g)
    return o.reshape(B, S, D_HG).astype(dt)


def hier_moe(h, w_router_group, b_router_group, w_router_expert, b_router_expert, w1, w3, w2):
    B, S, D = h.shape
    N = B * S
    xt = h.reshape(N, D)
    g_logits = (xt @ w_router_group).astype(jnp.float32) + b_router_group.astype(jnp.float32)
    g_prob = jax.nn.softmax(g_logits, axis=-1)
    grp = jnp.argmax(g_logits, axis=-1).astype(jnp.int32)
    p_grp = jnp.take_along_axis(g_prob, grp[:, None], axis=-1)
    e_logits = ((xt @ w_router_expert).astype(jnp.float32) + b_router_expert.astype(jnp.float32)).reshape(N, N_GROUPS, EXPERTS_PER_GROUP)
    e_in = jnp.take_along_axis(e_logits, grp[:, None, None], axis=1)[:, 0]
    top_v, top_i = lax.top_k(e_in, TOP_K)
    gate = p_grp * jax.nn.softmax(top_v, axis=-1)
    expert = grp[:, None] * EXPERTS_PER_GROUP + top_i.astype(jnp.int32)

    A = N * TOP_K
    eid = expert.reshape(A)
    tid = jnp.repeat(jnp.arange(N, dtype=jnp.int32), TOP_K)
    gflat = gate.reshape(A)
    counts = jnp.zeros((N_EXPERTS,), jnp.int32).at[eid].add(1)
    padded = ((counts + MOE_BLOCK - 1) // MOE_BLOCK) * MOE_BLOCK
    pad_end = jnp.cumsum(padded)
    pad_start = pad_end - padded
    raw_start = jnp.cumsum(counts) - counts
    order = jnp.argsort(eid)
    se = eid[order]
    dest = pad_start[se] + (jnp.arange(A, dtype=jnp.int32) - raw_start[se])
    nb = (A + N_EXPERTS * (MOE_BLOCK - 1) + MOE_BLOCK - 1) // MOE_BLOCK
    P = nb * MOE_BLOCK
    slot_tok = jnp.full((P,), N, jnp.int32).at[dest].set(tid[order])
    slot_gate = jnp.zeros((P,), jnp.float32).at[dest].set(gflat[order])
    block_start = jnp.arange(nb, dtype=jnp.int32) * MOE_BLOCK
    block_exp = jnp.minimum(jnp.searchsorted(pad_end, block_start, side='right'), N_EXPERTS - 1).astype(jnp.int32)
    x_pad = jnp.concatenate([xt, jnp.zeros((1, D), xt.dtype)], axis=0)
    xb = x_pad[slot_tok].reshape(nb, MOE_BLOCK, D)

    def run_block(args):
        xblk, e = args
        return (jax.nn.silu(xblk @ w1[e]) * (xblk @ w3[e])) @ w2[e]

    yb = lax.map(run_block, (xb, block_exp)).reshape(P, D)
    out = jnp.zeros((N + 1, D), yb.dtype).at[slot_tok].add(yb * slot_gate[:, None].astype(yb.dtype))
    return out[:N].reshape(B, S, D)


def setup_inputs(seed: int = 0) -> dict:
    key = jax.random.key(seed)
    ks = jax.random.split(key, 24)

    def nrm(k, shape, scale):
        return jax.random.normal(k, shape, jnp.float32) * scale

    x = nrm(ks[0], (BATCH, SEQ, D_MODEL), 1.0)
    positions = jax.random.randint(ks[1], (BATCH, 1), 0, 65536, dtype=jnp.int32) + jnp.arange(SEQ, dtype=jnp.int32)[None, :]
    col_scale = jnp.concatenate([
        jnp.ones((MLA_Q_RANK + MLA_KV_RANK + MLA_ROPE + HG_HEADS * HG_DK,), jnp.float32),
        jnp.full((HG_HEADS * HG_DV,), BETA, jnp.float32),
        jnp.ones((HG_HEADS * HG_DK + HG_HEADS * HG_DV,), jnp.float32)])
    w_in = nrm(ks[2], (DEPTH, D_MODEL, IN_COLS), D_MODEL ** -0.5) * col_scale
    q_norm_g = 1.0 + nrm(ks[3], (DEPTH, MLA_Q_RANK), 0.02)
    w_uq = nrm(ks[4], (DEPTH, MLA_Q_RANK, MLA_HEADS * (MLA_NOPE + MLA_ROPE)), MLA_Q_RANK ** -0.5)
    kv_norm_g = 1.0 + nrm(ks[5], (DEPTH, MLA_KV_RANK), 0.02)
    kv_scale = jnp.concatenate([jnp.ones((MLA_NOPE,), jnp.float32), jnp.full((MLA_V,), BETA, jnp.float32)])
    w_ukv = (nrm(ks[6], (DEPTH, MLA_KV_RANK, MLA_HEADS, MLA_NOPE + MLA_V), MLA_KV_RANK ** -0.5) * kv_scale).reshape(DEPTH, MLA_KV_RANK, MLA_HEADS * (MLA_NOPE + MLA_V))
    mla_out_g = 1.0 + nrm(ks[7], (DEPTH, D_MLA), 0.02)
    hg_lb_logits = nrm(ks[8], (DEPTH + 1, HG_HEADS * HG_DK), 0.5)
    hg_norm_g = 1.0 + nrm(ks[9], (DEPTH, HG_DV), 0.02)
    w_out = nrm(ks[10], (DEPTH, D_MIX, D_MODEL), D_MIX ** -0.5 * BETA)
    ln1_g = 1.0 + nrm(ks[11], (DEPTH, D_MODEL), 0.02)
    ln1_b = nrm(ks[12], (DEPTH, D_MODEL), 0.02)
    w_router_group = nrm(ks[13], (DEPTH, D_MODEL, N_GROUPS), D_MODEL ** -0.5)
    b_router_group = nrm(ks[14], (DEPTH, N_GROUPS), 0.01)
    w_router_expert = nrm(ks[15], (DEPTH, D_MODEL, N_EXPERTS), D_MODEL ** -0.5)
    b_router_expert = nrm(ks[16], (DEPTH, N_EXPERTS), 0.01)
    w1 = nrm(ks[17], (DEPTH, N_EXPERTS, D_MODEL, D_EXPERT), D_MODEL ** -0.5 * BETA)
    w3 = nrm(ks[18], (DEPTH, N_EXPERTS, D_MODEL, D_EXPERT), D_MODEL ** -0.5 * BETA)
    w2 = nrm(ks[19], (DEPTH, N_EXPERTS, D_EXPERT, D_MODEL), D_EXPERT ** -0.5 * BETA)
    ln2_g = 1.0 + nrm(ks[20], (DEPTH, D_MODEL), 0.02)
    ln2_b = nrm(ks[21], (DEPTH, D_MODEL), 0.02)
    return {"x": x, "positions": positions, "w_in": w_in, "q_norm_g": q_norm_g, "w_uq": w_uq,
            "kv_norm_g": kv_norm_g, "w_ukv": w_ukv, "mla_out_g": mla_out_g, "hg_lb_logits": hg_lb_logits,
            "hg_norm_g": hg_norm_g, "w_out": w_out, "ln1_g": ln1_g, "ln1_b": ln1_b,
            "w_router_group": w_router_group, "b_router_group": b_router_group,
            "w_router_expert": w_router_expert, "b_router_expert": b_router_expert,
            "w1": w1, "w3": w3, "w2": w2, "ln2_g": ln2_g, "ln2_b": ln2_b}


def reference(x, positions, w_in, q_norm_g, w_uq, kv_norm_g, w_ukv, mla_out_g, hg_lb_logits, hg_norm_g,
              w_out, ln1_g, ln1_b, w_router_group, b_router_group, w_router_expert, b_router_expert,
              w1, w3, w2, ln2_g, ln2_b):
    split_idx = [int(v) for v in np.cumsum(IN_SIZES)[:-1]]
    h = x
    for l in range(DEPTH):
        proj = h @ w_in[l]
        c_q, c_kv, k_r, f_pre, i_in, q_in, g_in = jnp.split(proj, split_idx, axis=-1)
        a = rms_norm(mla_group(c_q, c_kv, k_r, positions, q_norm_g[l], w_uq[l], kv_norm_g[l], w_ukv[l]), mla_out_g[l])
        r = hgrn2_group(f_pre, i_in, q_in, g_in, hg_lb_logits, l, hg_norm_g[l])
        mix = jnp.concatenate([a, r], axis=-1) @ w_out[l]
        h = layer_norm(ALPHA * h + mix, ln1_g[l], ln1_b[l])
        moe = hier_moe(h, w_router_group[l], b_router_group[l], w_router_expert[l], b_router_expert[l], w1[l], w3[l], w2[l])
        h = layer_norm(ALPHA * h + moe, ln2_g[l], ln2_b[l])
    return h
```

```python
import functools

import jax
import jax.numpy as jnp
from jax import lax
from jax.experimental import pallas as pl
from jax.experimental.pallas import tpu as pltpu

F32 = jnp.float32
BF16 = jnp.bfloat16

D_MODEL = 1024
CHUNK = 64
SUB = 16
N_SUB = CHUNK // SUB

MLA_HEADS = 8
MLA_NOPE = 64
MLA_ROPE = 32
MLA_V = 64
MLA_Q_RANK = 256
MLA_KV_RANK = 128
ROPE_THETA = 10000.0
D_MLA = MLA_HEADS * MLA_V
D_LAT = MLA_KV_RANK + MLA_ROPE

HG_HEADS = 4
HG_DK = 128
HG_DV = 128
D_HG = HG_HEADS * HG_DV

N_GROUPS = 4
EXPERTS_PER_GROUP = 8
N_EXPERTS = N_GROUPS * EXPERTS_PER_GROUP
TOP_K = 2
D_EXPERT = 512

RMS_EPS = 1e-6
LN_EPS = 1e-5
DEPTH = 1
ALPHA = (2.0 * DEPTH) ** 0.25

LANES = 128
VMEM_LIMIT = 48 * 1024 * 1024

TM_PROJ = 512
TQ = 128
TK = 256
TC_HG = 512
TB_MOE = 256
TM_COMB = 256

C_CQ = 0
C_CKV = MLA_Q_RANK
C_KR = C_CKV + MLA_KV_RANK
C_KRS = C_KR + MLA_ROPE
C_HG = 512
W1_COLS = C_HG + 4 * D_HG
Q_LAT = MLA_HEADS * MLA_KV_RANK
Q_ROPE = MLA_HEADS * MLA_ROPE
WQ_COLS = Q_LAT + 2 * Q_ROPE


def _cparams(sem):
    return pltpu.CompilerParams(dimension_semantics=sem, vmem_limit_bytes=VMEM_LIMIT)


def _dot(a, b):
    return jnp.dot(a, b, preferred_element_type=F32)


def _dot_nt(a, b):
    return lax.dot_general(a, b, (((1,), (1,)), ((), ())), preferred_element_type=F32)


def _dot_tn(a, b):
    return lax.dot_general(a, b, (((0,), (0,)), ((), ())), preferred_element_type=F32)


def _split3(x):
    hi = x.astype(BF16)
    r1 = x - hi.astype(F32)
    mid = r1.astype(BF16)
    lo = (r1 - mid.astype(F32)).astype(BF16)
    return hi, mid, lo


def _rope_body(pos_ref, freq_ref, sign_ref, cc_ref, ss_ref):
    ang = pos_ref[...] * freq_ref[...]
    cc_ref[...] = jnp.cos(ang)
    ss_ref[...] = jnp.sin(ang) * sign_ref[...]


def _rope_tables(positions):
    n = positions.size
    per_row = LANES // MLA_ROPE
    rows = n // per_row
    pos = jnp.repeat(positions.reshape(n).astype(F32), MLA_ROPE).reshape(rows, LANES)
    inv_freq = ROPE_THETA ** (-jnp.arange(0, MLA_ROPE, 2, dtype=F32) / MLA_ROPE)
    freq32 = jnp.concatenate([inv_freq, inv_freq])
    sign32 = jnp.concatenate([-jnp.ones((MLA_ROPE // 2,), F32), jnp.ones((MLA_ROPE // 2,), F32)])
    freq = jnp.tile(freq32, per_row).reshape(1, LANES)
    sign = jnp.tile(sign32, per_row).reshape(1, LANES)
    tr = min(rows, 512)
    cc, ss = pl.pallas_call(
        _rope_body,
        grid=(rows // tr,),
        in_specs=[pl.BlockSpec((tr, LANES), lambda i: (i, 0)),
                  pl.BlockSpec((1, LANES), lambda i: (0, 0)),
                  pl.BlockSpec((1, LANES), lambda i: (0, 0))],
        out_specs=[pl.BlockSpec((tr, LANES), lambda i: (i, 0))] * 2,
        out_shape=[jax.ShapeDtypeStruct((rows, LANES), F32)] * 2,
        compiler_params=_cparams(("parallel",)),
        name="rope_tables",
    )(pos, freq, sign)
    return cc.reshape(n, MLA_ROPE), ss.reshape(n, MLA_ROPE)


def _fold_body(wq_ref, wk_ref, o_ref):
    a = _split3(wq_ref[0])
    b = _split3(wk_ref[0])
    acc = jnp.zeros((MLA_Q_RANK, MLA_KV_RANK), F32)
    for i in range(3):
        for j in range(3 - i):
            acc = acc + _dot_nt(a[i], b[j])
    o_ref[0] = acc


def _fold_q(wuq_nope, wuk):
    return pl.pallas_call(
        _fold_body,
        grid=(MLA_HEADS,),
        in_specs=[pl.BlockSpec((1, MLA_Q_RANK, MLA_NOPE), lambda h: (h, 0, 0)),
                  pl.BlockSpec((1, MLA_KV_RANK, MLA_NOPE), lambda h: (h, 0, 0))],
        out_specs=pl.BlockSpec((1, MLA_Q_RANK, MLA_KV_RANK), lambda h: (h, 0, 0)),
        out_shape=jax.ShapeDtypeStruct((MLA_HEADS, MLA_Q_RANK, MLA_KV_RANK), F32),
        compiler_params=_cparams(("parallel",)),
        name="fold_q",
    )(wuq_nope, wuk)


def _rms(x, g):
    return x * lax.rsqrt(jnp.mean(x * x, axis=-1, keepdims=True) + RMS_EPS) * g


def _in_proj_body(x_ref, w1_ref, wq_ref, qg_ref, kvg_ref, cc8_ref, ss8_ref, cc_ref, ss_ref,
                  q_ref, k_ref, h_ref):
    xb = x_ref[...].astype(BF16)
    h_ref[...] = _dot(xb, w1_ref[:, C_HG:])
    p = _dot(xb, w1_ref[:, :C_HG])
    scale = (MLA_NOPE + MLA_ROPE) ** -0.5

    cqn = _rms(p[:, C_CQ:C_CKV], qg_ref[...])
    pq = _dot(cqn.astype(BF16), wq_ref[...])
    qr = (pq[:, Q_LAT:Q_LAT + Q_ROPE] * cc8_ref[...] + pq[:, Q_LAT + Q_ROPE:] * ss8_ref[...]) * scale
    for h in range(MLA_HEADS):
        lat = pq[:, h * MLA_KV_RANK:(h + 1) * MLA_KV_RANK] * scale
        q_ref[0, h] = jnp.concatenate([lat, qr[:, h * MLA_ROPE:(h + 1) * MLA_ROPE]], axis=1).astype(BF16)

    ckvn = _rms(p[:, C_CKV:C_KR], kvg_ref[...])
    kr = p[:, C_KR:C_KRS] * cc_ref[...] + p[:, C_KRS:C_KRS + MLA_ROPE] * ss_ref[...]
    k_ref[0] = jnp.concatenate([ckvn, kr], axis=1).astype(BF16)


def _in_proj(x, w1, wq, qg, kvg, cc8, ss8, cc, ss):
    b, s, _ = x.shape
    n = b * s
    tm = TM_PROJ
    spb = s // tm
    x2 = x.reshape(n, D_MODEL)
    row = lambda bi, i: (bi * spb + i, 0)
    const = lambda bi, i: (0, 0)
    return pl.pallas_call(
        _in_proj_body,
        grid=(b, spb),
        in_specs=[pl.BlockSpec((tm, D_MODEL), row),
                  pl.BlockSpec((D_MODEL, W1_COLS), const),
                  pl.BlockSpec((MLA_Q_RANK, WQ_COLS), const),
                  pl.BlockSpec((1, MLA_Q_RANK), const),
                  pl.BlockSpec((1, MLA_KV_RANK), const),
                  pl.BlockSpec((tm, Q_ROPE), row),
                  pl.BlockSpec((tm, Q_ROPE), row),
                  pl.BlockSpec((tm, MLA_ROPE), row),
                  pl.BlockSpec((tm, MLA_ROPE), row)],
        out_specs=[pl.BlockSpec((1, MLA_HEADS, tm, D_LAT), lambda bi, i: (bi, 0, i, 0)),
                   pl.BlockSpec((1, tm, D_LAT), lambda bi, i: (bi, i, 0)),
                   pl.BlockSpec((tm, 4 * D_HG), row)],
        out_shape=[jax.ShapeDtypeStruct((b, MLA_HEADS, s, D_LAT), BF16),
                   jax.ShapeDtypeStruct((b, s, D_LAT), BF16),
                   jax.ShapeDtypeStruct((n, 4 * D_HG), F32)],
        compiler_params=_cparams(("parallel", "parallel")),
        name="in_proj",
    )(x2, w1, wq, qg, kvg, cc8, ss8, cc, ss)


def _attn_body(q_ref, k_ref, wuv_ref, og_ref, o_ref, m_scr, l_scr, acc_scr):
    qi = pl.program_id(1)
    rows = MLA_HEADS * TQ
    q = q_ref[0].reshape(rows, D_LAT)
    q0 = qi * TQ

    m_scr[...] = jnp.full((rows, LANES), -jnp.inf, F32)
    l_scr[...] = jnp.zeros((rows, LANES), F32)
    acc_scr[...] = jnp.zeros((rows, MLA_KV_RANK), F32)

    def step(j, masked):
        k0 = pl.multiple_of(j * TK, TK)
        k = k_ref[0, pl.ds(k0, TK), :]
        s = _dot_nt(q, k)
        if masked:
            qpos = q0 + (lax.broadcasted_iota(jnp.int32, (rows, TK), 0) & (TQ - 1))
            kpos = k0 + lax.broadcasted_iota(jnp.int32, (rows, TK), 1)
            s = jnp.where((kpos >> 6) <= (qpos >> 6), s, -jnp.inf)
        m_prev = m_scr[...]
        m_next = jnp.maximum(m_prev, jnp.max(s, axis=1, keepdims=True))
        p = jnp.exp(s - jnp.concatenate([m_next] * (TK // LANES), axis=1))
        alpha = jnp.exp(m_prev - m_next)
        l_scr[...] = alpha * l_scr[...] + jnp.sum(p, axis=1, keepdims=True)
        acc_scr[...] = acc_scr[...] * alpha + _dot(p.astype(BF16), k[:, :MLA_KV_RANK])
        m_scr[...] = m_next

    n_full = q0 // TK

    def full_step(j, carry):
        step(j, False)
        return carry

    lax.fori_loop(0, n_full, full_step, 0)
    step(n_full, True)

    o = acc_scr[...] / l_scr[...]
    o_cat = jnp.concatenate([o[h * TQ:(h + 1) * TQ] for h in range(MLA_HEADS)], axis=1)
    a = _dot(o_cat.astype(BF16), wuv_ref[...])
    o_ref[0] = _rms(a, og_ref[...]).astype(BF16)


def _mla_attn(q, k, wuv_bd, og):
    b, _, s, _ = q.shape
    assert TK == 2 * TQ and CHUNK * 2 == TQ
    return pl.pallas_call(
        _attn_body,
        grid=(b, s // TQ),
        in_specs=[pl.BlockSpec((1, MLA_HEADS, TQ, D_LAT), lambda bi, i: (bi, 0, i, 0)),
                  pl.BlockSpec((1, s, D_LAT), lambda bi, i: (bi, 0, 0)),
                  pl.BlockSpec((MLA_HEADS * MLA_KV_RANK, D_MLA), lambda bi, i: (0, 0)),
                  pl.BlockSpec((1, D_MLA), lambda bi, i: (0, 0))],
        out_specs=pl.BlockSpec((1, TQ, D_MLA), lambda bi, i: (bi, i, 0)),
        out_shape=jax.ShapeDtypeStruct((b, s, D_MLA), BF16),
        scratch_shapes=[pltpu.VMEM((MLA_HEADS * TQ, LANES), F32),
                        pltpu.VMEM((MLA_HEADS * TQ, LANES), F32),
                        pltpu.VMEM((MLA_HEADS * TQ, MLA_KV_RANK), F32)],
        compiler_params=_cparams(("parallel", "parallel")),
        name="mla_attn",
    )(q, k, wuv_bd, og)


_PAIRS = [(i, j) for i in range(1, N_SUB) for j in range(i)]


def _hgrn_body(lbl_ref, gn_ref, f_ref, v_ref, q_ref, g_ref, o_ref, st_scr):
    @pl.when(pl.program_id(2) == 0)
    def _():
        st_scr[...] = jnp.zeros((HG_DV, HG_DK), F32)

    lg = lbl_ref[...]
    e = jnp.exp(lg - jnp.max(lg, axis=0, keepdims=True))
    lb = e[0:1] / jnp.sum(e, axis=0, keepdims=True)

    ri = lax.broadcasted_iota(jnp.int32, (CHUNK, CHUNK), 0)
    ci = lax.broadcasted_iota(jnp.int32, (CHUNK, CHUNK), 1)
    tri = jnp.where(ci <= ri, 1.0, 0.0).astype(BF16)

    n_pair = len(_PAIRS)
    pr = lax.broadcasted_iota(jnp.int32, (n_pair * SUB, CHUNK), 0)
    pc = lax.broadcasted_iota(jnp.int32, (n_pair * SUB, CHUNK), 1)
    pair_j = jnp.zeros((n_pair * SUB, CHUNK), jnp.int32)
    for idx, (_, j) in enumerate(_PAIRS):
        pair_j = jnp.where((pr >= idx * SUB) & (pr < (idx + 1) * SUB), j, pair_j)
    pair_mask = (pc >> 4) == pair_j

    lane_c = lax.broadcasted_iota(jnp.int32, (SUB, CHUNK), 1)
    row_s = lax.broadcasted_iota(jnp.int32, (SUB, HG_DK), 0)

    def chunk(c, carry):
        r0 = pl.multiple_of(c * CHUNK, CHUNK)
        fp = f_ref[pl.ds(r0, CHUNK), :]
        f = lb + (1.0 - lb) * (1.0 / (1.0 + jnp.exp(-fp)))
        lf = jnp.log(f)
        kk = 1.0 - f
        qq = q_ref[pl.ds(r0, CHUNK), :]
        vb = v_ref[pl.ds(r0, CHUNK), :].astype(BF16)

        hi, mid, lo = _split3(lf)
        bcum = _dot(tri, hi) + _dot(tri, mid) + _dot(tri, lo)

        ends = [bcum[(j + 1) * SUB - 1:(j + 1) * SUB, :] for j in range(N_SUB)]
        starts = [jnp.zeros((1, HG_DK), F32)] + ends[:-1]
        b_last = ends[-1]

        qt, kt, qh, kh, adiag = [], [], [], [], []
        for i in range(N_SUB):
            sl = slice(i * SUB, (i + 1) * SUB)
            bi, qi, ki = bcum[sl], qq[sl], kk[sl]
            qt_i = qi * jnp.exp(bi - starts[i])
            kt_i = ki * jnp.exp(ends[i] - bi)
            qt.append(qt_i)
            kt.append(kt_i)
            qh.append(qt_i * jnp.exp(starts[i]))
            kh.append(kt_i * jnp.exp(b_last - ends[i]))
            a_i = jnp.zeros((SUB, CHUNK), F32)
            for s_ in range(SUB):
                dec = jnp.exp(jnp.where(row_s >= s_, bi - bi[s_:s_ + 1, :], -jnp.inf))
                col = jnp.sum(qi * (ki[s_:s_ + 1, :] * dec), axis=1, keepdims=True)
                a_i = jnp.where(lane_c == i * SUB + s_, col, a_i)
            adiag.append(a_i)

        kt_all = jnp.concatenate(kt, axis=0).astype(BF16)
        qp = jnp.concatenate([qt[i] * jnp.exp(starts[i] - ends[j]) for (i, j) in _PAIRS], axis=0)
        s_off = jnp.where(pair_mask, _dot_nt(qp.astype(BF16), kt_all), 0.0)
        a_all = jnp.concatenate(adiag + [s_off], axis=0).astype(BF16)
        r = _dot(a_all, vb)

        st = st_scr[...]
        qh_all = jnp.concatenate(qh, axis=0).astype(BF16)
        o_inter = _dot_nt(qh_all, st.astype(BF16))
        outs = []
        for i in range(N_SUB):
            o_i = r[i * SUB:(i + 1) * SUB] + o_inter[i * SUB:(i + 1) * SUB]
            for idx, (pi, _) in enumerate(_PAIRS):
                if pi == i:
                    o_i = o_i + r[CHUNK + idx * SUB:CHUNK + (idx + 1) * SUB]
            outs.append(o_i)
        o = jnp.concatenate(outs, axis=0)

        kh_all = jnp.concatenate(kh, axis=0).astype(BF16)
        st_scr[...] = st * jnp.exp(b_last) + _dot_tn(vb, kh_all)

        gg = g_ref[pl.ds(r0, CHUNK), :]
        y = _rms(o, gn_ref[...]) * (gg * (1.0 / (1.0 + jnp.exp(-gg))))
        o_ref[pl.ds(r0, CHUNK), :] = y.astype(BF16)
        return carry

    lax.fori_loop(0, TC_HG // CHUNK, chunk, 0)


def _hgrn2(hraw, lb_logits, gn, b, s):
    n = b * s
    spb = s // TC_HG
    nh = HG_HEADS

    def col(off):
        return lambda bi, h, i: (bi * spb + i, off * nh + h)

    return pl.pallas_call(
        _hgrn_body,
        grid=(b, nh, spb),
        in_specs=[pl.BlockSpec((DEPTH + 1, HG_DK), lambda bi, h, i: (0, h)),
                  pl.BlockSpec((1, HG_DV), lambda bi, h, i: (0, 0)),
                  pl.BlockSpec((TC_HG, HG_DK), col(0)),
                  pl.BlockSpec((TC_HG, HG_DV), col(1)),
                  pl.BlockSpec((TC_HG, HG_DK), col(2)),
                  pl.BlockSpec((TC_HG, HG_DV), col(3))],
        out_specs=pl.BlockSpec((TC_HG, HG_DV), lambda bi, h, i: (bi * spb + i, h)),
        out_shape=jax.ShapeDtypeStruct((n, D_HG), BF16),
        scratch_shapes=[pltpu.VMEM((HG_DV, HG_DK), F32)],
        compiler_params=_cparams(("parallel", "parallel", "arbitrary")),
        name="hgrn2",
    )(lb_logits, gn, hraw, hraw, hraw, hraw)


def _layer_norm(x, g, b):
    mu = jnp.mean(x, axis=-1, keepdims=True)
    xc = x - mu
    var = jnp.mean(xc * xc, axis=-1, keepdims=True)
    return xc * lax.rsqrt(var + LN_EPS) * g + b


def _mix_body(x_ref, a_ref, r_ref, wa_ref, wr_ref, g_ref, b_ref, wrt_ref, brt_ref, h_ref, rt_ref):
    mix = _dot(a_ref[...], wa_ref[...]) + _dot(r_ref[...], wr_ref[...])
    h = _layer_norm(ALPHA * x_ref[...] + mix, g_ref[...], b_ref[...])
    h_ref[...] = h

    hs = _split3(h)
    logits = brt_ref[...]
    for i in range(3):
        for j in range(3 - i):
            logits = logits + _dot(hs[i], wrt_ref[j])
    lane = lax.broadcasted_iota(jnp.int32, logits.shape, 1)
    neg = -jnp.inf

    gl = jnp.where(lane < N_GROUPS, logits, neg)
    gmax = jnp.max(gl, axis=1, keepdims=True)
    grp = jnp.min(jnp.where(gl == gmax, lane, LANES), axis=1, keepdims=True)
    p_grp = 1.0 / jnp.sum(jnp.exp(gl - gmax), axis=1, keepdims=True)

    eidx = lane - N_GROUPS
    in_grp = (eidx >= grp * EXPERTS_PER_GROUP) & (eidx < (grp + 1) * EXPERTS_PER_GROUP)
    el = jnp.where(in_grp, logits, neg)
    v1 = jnp.max(el, axis=1, keepdims=True)
    i1 = jnp.min(jnp.where(el == v1, eidx, LANES), axis=1, keepdims=True)
    el2 = jnp.where(eidx == i1, neg, el)
    v2 = jnp.max(el2, axis=1, keepdims=True)
    i2 = jnp.min(jnp.where(el2 == v2, eidx, LANES), axis=1, keepdims=True)
    e2 = jnp.exp(v2 - v1)
    g1 = p_grp / (1.0 + e2)
    g2 = p_grp * e2 / (1.0 + e2)

    out = jnp.where(lane == 0, i1.astype(F32),
                    jnp.where(lane == 1, i2.astype(F32),
                              jnp.where(lane == 2, g1, jnp.where(lane == 3, g2, 0.0))))
    rt_ref[...] = out


def _mix_router(x2, a, r, wa, wr, g, bta, wrt3, brt):
    n = x2.shape[0]
    tm = TM_PROJ
    row = lambda i: (i, 0)
    const = lambda i: (0, 0)
    return pl.pallas_call(
        _mix_body,
        grid=(n // tm,),
        in_specs=[pl.BlockSpec((tm, D_MODEL), row),
                  pl.BlockSpec((tm, D_MLA), row),
                  pl.BlockSpec((tm, D_HG), row),
                  pl.BlockSpec((D_MLA, D_MODEL), const),
                  pl.BlockSpec((D_HG, D_MODEL), const),
                  pl.BlockSpec((1, D_MODEL), const),
                  pl.BlockSpec((1, D_MODEL), const),
                  pl.BlockSpec((3, D_MODEL, LANES), lambda i: (0, 0, 0)),
                  pl.BlockSpec((1, LANES), const)],
        out_specs=[pl.BlockSpec((tm, D_MODEL), row),
                   pl.BlockSpec((tm, LANES), row)],
        out_shape=[jax.ShapeDtypeStruct((n, D_MODEL), F32),
                   jax.ShapeDtypeStruct((n, LANES), F32)],
        compiler_params=_cparams(("parallel",)),
        name="mix_router",
    )(x2, a, r, wa, wr, g, bta, wrt3, brt)


def _row_gather(src_hbm, dst, sem, ids_ref, n_rows):
    def body(r, carry):
        tok = ids_ref[0, 0, r]
        pltpu.make_async_copy(src_hbm.at[pl.ds(tok, 1)], dst.at[pl.ds(r, 1)], sem).start()
        return carry

    lax.fori_loop(0, n_rows, body, 0, unroll=8)


def _moe_body(bexp_ref, nused_ref, ids_cur, ids_nxt, h_hbm, w1_ref, w3_ref, w2_ref, y_ref,
              xbuf, w1b, w3b, w2b, sem):
    i = pl.program_id(0)
    n_used = nused_ref[0]
    slot = i % 2

    @pl.when(i == 0)
    def _():
        _row_gather(h_hbm, xbuf.at[0], sem.at[0], ids_cur, TB_MOE)

    @pl.when(i + 1 < n_used)
    def _():
        _row_gather(h_hbm, xbuf.at[1 - slot], sem.at[1 - slot], ids_nxt, TB_MOE)

    prev = bexp_ref[jnp.maximum(i - 1, 0)]

    @pl.when((i == 0) | (bexp_ref[i] != prev))
    def _():
        w1b[...] = w1_ref[0].astype(BF16)
        w3b[...] = w3_ref[0].astype(BF16)
        w2b[...] = w2_ref[0].astype(BF16)

    @pl.when(i < n_used)
    def _():
        pltpu.make_async_copy(h_hbm.at[pl.ds(0, TB_MOE)], xbuf.at[slot], sem.at[slot]).wait()
        xb = xbuf[slot].astype(BF16)
        u = _dot(xb, w1b[...])
        t = _dot(xb, w3b[...])
        hmid = (u * (1.0 / (1.0 + jnp.exp(-u)))) * t
        y_ref[...] = _dot(hmid.astype(BF16), w2b[...])

    @pl.when(i >= n_used)
    def _():
        y_ref[...] = jnp.zeros((TB_MOE, D_MODEL), F32)


def _moe_experts(h1, slot_tok3, block_exp, n_used, w1, w3, w2):
    nblk = slot_tok3.shape[0]
    grid_spec = pltpu.PrefetchScalarGridSpec(
        num_scalar_prefetch=2,
        grid=(nblk,),
        in_specs=[pl.BlockSpec((1, 1, TB_MOE), lambda i, be, nu: (i, 0, 0), memory_space=pltpu.SMEM),
                  pl.BlockSpec((1, 1, TB_MOE), lambda i, be, nu: (jnp.minimum(i + 1, nblk - 1), 0, 0),
                               memory_space=pltpu.SMEM),
                  pl.BlockSpec(memory_space=pl.ANY),
                  pl.BlockSpec((1, D_MODEL, D_EXPERT), lambda i, be, nu: (be[i], 0, 0)),
                  pl.BlockSpec((1, D_MODEL, D_EXPERT), lambda i, be, nu: (be[i], 0, 0)),
                  pl.BlockSpec((1, D_EXPERT, D_MODEL), lambda i, be, nu: (be[i], 0, 0))],
        out_specs=pl.BlockSpec((TB_MOE, D_MODEL), lambda i, be, nu: (i, 0)),
        scratch_shapes=[pltpu.VMEM((2, TB_MOE, D_MODEL), F32),
                        pltpu.VMEM((D_MODEL, D_EXPERT), BF16),
                        pltpu.VMEM((D_MODEL, D_EXPERT), BF16),
                        pltpu.VMEM((D_EXPERT, D_MODEL), BF16),
                        pltpu.SemaphoreType.DMA((2,))],
    )
    return pl.pallas_call(
        _moe_body,
        grid_spec=grid_spec,
        out_shape=jax.ShapeDtypeStruct((nblk * TB_MOE, D_MODEL), F32),
        compiler_params=_cparams(("arbitrary",)),
        name="moe_experts",
    )(block_exp, n_used, slot_tok3, slot_tok3, h1, w1, w3, w2)


def _comb_body(p0_cur, p1_cur, p0_nxt, p1_nxt, y_hbm, h_ref, rt_ref, g_ref, b_ref, o_ref, ybuf, sem):
    i = pl.program_id(0)
    nsteps = pl.num_programs(0)
    slot = i % 2

    @pl.when(i == 0)
    def _():
        _row_gather(y_hbm, ybuf.at[0, 0], sem.at[0], p0_cur, TM_COMB)
        _row_gather(y_hbm, ybuf.at[0, 1], sem.at[0], p1_cur, TM_COMB)

    @pl.when(i + 1 < nsteps)
    def _():
        _row_gather(y_hbm, ybuf.at[1 - slot, 0], sem.at[1 - slot], p0_nxt, TM_COMB)
        _row_gather(y_hbm, ybuf.at[1 - slot, 1], sem.at[1 - slot], p1_nxt, TM_COMB)

    pltpu.make_async_copy(y_hbm.at[pl.ds(0, TM_COMB)], ybuf.at[slot, 0], sem.at[slot]).wait()
    pltpu.make_async_copy(y_hbm.at[pl.ds(0, TM_COMB)], ybuf.at[slot, 1], sem.at[slot]).wait()
    rt = rt_ref[...]
    moe = rt[:, 2:3] * ybuf[slot, 0] + rt[:, 3:4] * ybuf[slot, 1]
    o_ref[...] = _layer_norm(ALPHA * h_ref[...] + moe, g_ref[...], b_ref[...])


def _moe_combine(y, h1, rt, pos0, pos1, g, bta):
    n = h1.shape[0]
    tm = TM_COMB
    nsteps = n // tm
    cur = lambda i: (i, 0, 0)
    nxt = lambda i: (jnp.minimum(i + 1, nsteps - 1), 0, 0)
    row = lambda i: (i, 0)
    const = lambda i: (0, 0)
    smem = functools.partial(pl.BlockSpec, (1, 1, tm), memory_space=pltpu.SMEM)
    return pl.pallas_call(
        _comb_body,
        grid=(nsteps,),
        in_specs=[smem(cur), smem(cur), smem(nxt), smem(nxt),
                  pl.BlockSpec(memory_space=pl.ANY),
                  pl.BlockSpec((tm, D_MODEL), row),
                  pl.BlockSpec((tm, LANES), row),
                  pl.BlockSpec((1, D_MODEL), const),
                  pl.BlockSpec((1, D_MODEL), const)],
        out_specs=pl.BlockSpec((tm, D_MODEL), row),
        out_shape=jax.ShapeDtypeStruct((n, D_MODEL), F32),
        scratch_shapes=[pltpu.VMEM((2, 2, tm, D_MODEL), F32),
                        pltpu.SemaphoreType.DMA((2,))],
        compiler_params=_cparams(("arbitrary",)),
        name="moe_combine",
    )(pos0, pos1, pos0, pos1, y, h1, rt, g, bta)


def _route_tables(eid):
    n = eid.shape[0]
    a = n * TOP_K
    tb = TB_MOE
    eflat = eid.reshape(a)
    onehot = (eflat[:, None] == jnp.arange(N_EXPERTS, dtype=jnp.int32)[None, :]).astype(jnp.int32)
    csum = jnp.cumsum(onehot, axis=0)
    rank = jnp.take_along_axis(csum, eflat[:, None], axis=1)[:, 0] - 1
    counts = csum[-1]
    padded = ((counts + tb - 1) // tb) * tb
    pad_end = jnp.cumsum(padded)
    pad_start = pad_end - padded
    pos = pad_start[eflat] + rank
    nblk = (a + N_EXPERTS * (tb - 1)) // tb
    tok = jnp.arange(a, dtype=jnp.int32) // TOP_K
    slot_tok = jnp.zeros((nblk * tb,), jnp.int32).at[pos].set(tok)
    block_start = jnp.arange(nblk, dtype=jnp.int32) * tb
    block_exp = jnp.minimum(jnp.searchsorted(pad_end, block_start, side="right"), N_EXPERTS - 1).astype(jnp.int32)
    n_used = (pad_end[-1] // tb).astype(jnp.int32).reshape(1)
    pos2 = pos.reshape(n, TOP_K)
    return pos2[:, 0], pos2[:, 1], slot_tok.reshape(nblk, 1, tb), block_exp, n_used


def kernel(x, positions, w_in, q_norm_g, w_uq, kv_norm_g, w_ukv, mla_out_g, hg_lb_logits, hg_norm_g, w_out,
           ln1_g, ln1_b, w_router_group, b_router_group, w_router_expert, b_router_expert, w1, w3, w2,
           ln2_g, ln2_b):
    b, s, _ = x.shape
    n = b * s
    l = 0

    win = w_in[l]
    o_f = C_KR + MLA_ROPE
    half = MLA_ROPE // 2
    kr_cols = win[:, C_KR:C_KR + MLA_ROPE]
    kr_swap = jnp.concatenate([kr_cols[:, half:], kr_cols[:, :half]], axis=1)
    pad = jnp.zeros((D_MODEL, C_HG - C_KRS - MLA_ROPE), F32)
    w1f = jnp.concatenate([win[:, :C_KR], kr_cols, kr_swap, pad, win[:, o_f:]], axis=1).astype(BF16)

    wuq = w_uq[l].reshape(MLA_Q_RANK, MLA_HEADS, MLA_NOPE + MLA_ROPE)
    wukv = w_ukv[l].reshape(MLA_KV_RANK, MLA_HEADS, MLA_NOPE + MLA_V)
    wq_lat = _fold_q(jnp.transpose(wuq[:, :, :MLA_NOPE], (1, 0, 2)),
                     jnp.transpose(wukv[:, :, :MLA_NOPE], (1, 0, 2)))
    wq_lat = jnp.transpose(wq_lat, (1, 0, 2)).reshape(MLA_Q_RANK, Q_LAT)
    wq_rope = wuq[:, :, MLA_NOPE:]
    wq_rope_sw = jnp.concatenate([wq_rope[:, :, half:], wq_rope[:, :, :half]], axis=2)
    wq = jnp.concatenate([wq_lat, wq_rope.reshape(MLA_Q_RANK, Q_ROPE),
                          wq_rope_sw.reshape(MLA_Q_RANK, Q_ROPE)], axis=1).astype(BF16)

    wuv = jnp.transpose(wukv[:, :, MLA_NOPE:], (1, 0, 2))
    eye = jnp.eye(MLA_HEADS, dtype=F32)
    wuv_bd = (wuv[:, :, None, :] * eye[:, None, :, None]).reshape(MLA_HEADS * MLA_KV_RANK, D_MLA).astype(BF16)

    wout = w_out[l].astype(BF16)
    wrt = jnp.concatenate([w_router_group[l], w_router_expert[l],
                           jnp.zeros((D_MODEL, LANES - N_GROUPS - N_EXPERTS), F32)], axis=1)
    wrt3 = jnp.stack(_split3(wrt), axis=0)
    brt = jnp.concatenate([b_router_group[l], b_router_expert[l],
                           jnp.zeros((LANES - N_GROUPS - N_EXPERTS,), F32)]).reshape(1, LANES)

    cc, ss = _rope_tables(positions)
    cc8 = jnp.tile(cc, (1, MLA_HEADS))
    ss8 = jnp.tile(ss, (1, MLA_HEADS))

    q, k, hraw = _in_proj(x, w1f, wq, q_norm_g[l].reshape(1, -1), kv_norm_g[l].reshape(1, -1), cc8, ss8, cc, ss)

    a = _mla_attn(q, k, wuv_bd, mla_out_g[l].reshape(1, -1)).reshape(n, D_MLA)
    r = _hgrn2(hraw, hg_lb_logits, hg_norm_g[l].reshape(1, -1), b, s)

    x2 = x.reshape(n, D_MODEL)
    h1, rt = _mix_router(x2, a, r, wout[:D_MLA], wout[D_MLA:], ln1_g[l].reshape(1, -1), ln1_b[l].reshape(1, -1),
                         wrt3, brt)

    eid = rt[:, :TOP_K].astype(jnp.int32)
    pos0, pos1, slot_tok3, block_exp, n_used = _route_tables(eid)
    y = _moe_experts(h1, slot_tok3, block_exp, n_used, w1[l], w3[l], w2[l])
    nsteps = n // TM_COMB
    out = _moe_combine(y, h1, rt, pos0.reshape(nsteps, 1, TM_COMB), pos1.reshape(nsteps, 1, TM_COMB),
                       ln2_g[l].reshape(1, -1), ln2_b[l].reshape(1, -1))
    return out.reshape(b, s, D_MODEL)
```
